```python
import math
import jax, jax.numpy as jnp
from jax import lax
import numpy as np

D_MODEL = 1024
BATCH = 4
SEQ = 8192
DEPTH = 2

CTX_LEN = 256
GRID_W = 64
N_MOD = 9
ATTN_WIDTH = 512
CONV_WIDTH = 256
FOURIER_WIDTH = 256
MIX_WIDTH = ATTN_WIDTH + CONV_WIDTH + FOURIER_WIDTH
N_HEADS = 4
HEAD_DIM = 64
V_DIM = 2 * HEAD_DIM
QK_WIDTH = N_HEADS * 2 * HEAD_DIM
V_WIDTH = N_HEADS * V_DIM
ATTN_IN = 2 * QK_WIDTH + V_WIDTH
CONV_IN = 2 * CONV_WIDTH
IN_WIDTH = ATTN_IN + CONV_IN + FOURIER_WIDTH
CONV_K = 31
FOURIER_GROUPS = 4
FOURIER_GDIM = FOURIER_WIDTH // FOURIER_GROUPS
D_FF = 2816
ROPE_BASE = 10000.0
ROT_AXIS = HEAD_DIM // 2
Q_BLOCK = 128
EPS = 1e-6

kernel_name = "hybrid_diffattn_conformer_fnet_macaron_dit"


def rmsnorm(x, g):
    xf = x.astype(jnp.float32)
    y = xf * lax.rsqrt(jnp.mean(xf * xf, axis=-1, keepdims=True) + EPS)
    return (y * g.astype(jnp.float32)).astype(x.dtype)


def modulate(h, shift, scale):
    return h * (1 + scale) + shift


def swiglu(h, wg, wu, wd):
    return (jax.nn.silu(h @ wg) * (h @ wu)) @ wd


def axial_rope_tables(n):
    rows = n // GRID_W
    row = jnp.broadcast_to(jnp.arange(rows)[:, None], (rows, GRID_W)).reshape(-1)
    col = jnp.broadcast_to(jnp.arange(GRID_W)[None, :], (rows, GRID_W)).reshape(-1)
    inv = ROPE_BASE ** (-jnp.arange(0, ROT_AXIS, 2, dtype=jnp.float32) / ROT_AXIS)
    ang_r = row.astype(jnp.float32)[:, None] * inv
    ang_c = col.astype(jnp.float32)[:, None] * inv
    shp = (n, 1, 1, ROT_AXIS // 2)
    return (jnp.cos(ang_r).reshape(shp), jnp.sin(ang_r).reshape(shp),
            jnp.cos(ang_c).reshape(shp), jnp.sin(ang_c).reshape(shp))


def rotate(x, cos, sin):
    x1, x2 = jnp.split(x, 2, axis=-1)
    return jnp.concatenate([x1 * cos - x2 * sin, x2 * cos + x1 * sin], axis=-1)


def apply_axial_rope(x, rope):
    cos_r, sin_r, cos_c, sin_c = rope
    xf = x.astype(jnp.float32)
    out = jnp.concatenate([rotate(xf[..., :ROT_AXIS], cos_r, sin_r),
                           rotate(xf[..., ROT_AXIS:], cos_c, sin_c)], axis=-1)
    return out.astype(x.dtype)


def split_qkv(u):
    b, n = u.shape[0], u.shape[1]
    q = u[..., :QK_WIDTH].reshape(b, n, N_HEADS, 2, HEAD_DIM)
    k = u[..., QK_WIDTH:2 * QK_WIDTH].reshape(b, n, N_HEADS, 2, HEAD_DIM)
    v = u[..., 2 * QK_WIDTH:].reshape(b, n, N_HEADS, V_DIM)
    return q, k, v


def diff_scores_to_out(qb, k, v, lam):
    s = jnp.einsum('bhiqd,bhikd->bhiqk', qb, k, preferred_element_type=jnp.float32) * (HEAD_DIM ** -0.5)
    p = jax.nn.softmax(s, axis=-1)
    p_diff = p[:, :, 0] - lam * p[:, :, 1]
    return jnp.einsum('bhqk,bhkd->bhqd', p_diff.astype(v.dtype), v)


def diff_head_norm(o, g_subln, lam_init):
    o = rmsnorm(o, g_subln) * (1.0 - lam_init)
    return o.reshape(o.shape[0], o.shape[1], V_WIDTH)


def diff_attention(u_lat, u_ctx, lam, lam_init, g_subln, rope, need_ctx_out):
    b, n = u_lat.shape[0], u_lat.shape[1]
    q_l, k_l, v_l = split_qkv(u_lat)
    q_c, k_c, v_c = split_qkv(u_ctx)
    q_l = apply_axial_rope(q_l, rope)
    k_l = apply_axial_rope(k_l, rope)
    q_l = q_l.transpose(0, 2, 3, 1, 4)
    k_l = k_l.transpose(0, 2, 3, 1, 4)
    q_c = q_c.transpose(0, 2, 3, 1, 4)
    k_c = k_c.transpose(0, 2, 3, 1, 4)
    v_l = v_l.transpose(0, 2, 1, 3)
    v_c = v_c.transpose(0, 2, 1, 3)
    k_all = jnp.concatenate([k_l, k_c], axis=3)
    v_all = jnp.concatenate([v_l, v_c], axis=2)
    nb = n // Q_BLOCK
    q_blocks = jnp.moveaxis(q_l.reshape(b, N_HEADS, 2, nb, Q_BLOCK, HEAD_DIM), 3, 0)
    o = lax.map(lambda qb: diff_scores_to_out(qb, k_all, v_all, lam), q_blocks)
    o = o.transpose(1, 0, 3, 2, 4).reshape(b, n, N_HEADS, V_DIM)
    o_lat = diff_head_norm(o, g_subln, lam_init)
    o_ctx = None
    if need_ctx_out:
        oc = diff_scores_to_out(q_c, k_c, v_c, lam).transpose(0, 2, 1, 3)
        o_ctx = diff_head_norm(oc, g_subln, lam_init)
    return o_lat, o_ctx


def conformer_conv(u, w_dw, b_dw, g_ln, b_ln, w_pw, b_pw):
    a, gate = jnp.split(u, 2, axis=-1)
    y = a * jax.nn.sigmoid(gate)
    y = lax.conv_general_dilated(y, w_dw[:, None, :], window_strides=(1,),
                                 padding=[(CONV_K // 2, CONV_K // 2)],
                                 dimension_numbers=('NWC', 'WIO', 'NWC'),
                                 feature_group_count=CONV_WIDTH) + b_dw
    yf = y.astype(jnp.float32)
    mu = jnp.mean(yf, axis=-1, keepdims=True)
    var = jnp.mean(jnp.square(yf - mu), axis=-1, keepdims=True)
    yf = (yf - mu) * lax.rsqrt(var + EPS) * g_ln.astype(jnp.float32) + b_ln.astype(jnp.float32)
    y = jax.nn.silu(yf).astype(u.dtype)
    return y @ w_pw + b_pw


def fourier_mix(u, w_f, b_f):
    b, n = u.shape[0], u.shape[1]
    uf = u.astype(jnp.float32).reshape(b, n, FOURIER_GROUPS, FOURIER_GDIM)
    f = jnp.fft.fft2(uf, axes=(1, 3), norm='ortho').real
    f = f.reshape(b, n, FOURIER_WIDTH).astype(u.dtype)
    return f @ w_f + b_f


def ffn_sublayer(s, m, i, g, wg, wu, wd):
    h = modulate(rmsnorm(s, g), m[3 * i], m[3 * i + 1])
    return s + 0.5 * m[3 * i + 2] * swiglu(h, wg, wu, wd)


def hybrid_layer(x, ctx, mod_x, mod_c, p, lam_init, rope, need_ctx_out):
    mx = jnp.split(mod_x, N_MOD, axis=-1)
    mc = jnp.split(mod_c, N_MOD, axis=-1)
    x = ffn_sublayer(x, mx, 0, p['g_ffn1'], p['w1g'], p['w1u'], p['w1d'])
    ctx = ffn_sublayer(ctx, mc, 0, p['g_ffn1'], p['w1g'], p['w1u'], p['w1d'])
    h = modulate(rmsnorm(x, p['g_mix']), mx[3], mx[4])
    hc = modulate(rmsnorm(ctx, p['g_mix']), mc[3], mc[4])
    u = h @ p['w_in']
    uc = hc @ p['w_in']
    lam = (jnp.exp(jnp.sum(p['lq1'].astype(jnp.float32) * p['lk1'].astype(jnp.float32)))
           - jnp.exp(jnp.sum(p['lq2'].astype(jnp.float32) * p['lk2'].astype(jnp.float32)))
           + lam_init)
    a_lat, a_ctx = diff_attention(u[..., :ATTN_IN], uc[..., :ATTN_IN], lam, lam_init,
                                  p['g_subln'], rope, need_ctx_out)
    conv_args = (p['w_dw'], p['b_dw'], p['g_cln'], p['b_cln'], p['w_pw'], p['b_pw'])
    conv_lat = conformer_conv(u[..., ATTN_IN:ATTN_IN + CONV_IN], *conv_args)
    four_lat = fourier_mix(u[..., ATTN_IN + CONV_IN:], p['w_f'], p['b_f'])
    y = jnp.concatenate([a_lat, conv_lat, four_lat], axis=-1) @ p['w_out']
    x = x + mx[5] * y
    if need_ctx_out:
        conv_ctx = conformer_conv(uc[..., ATTN_IN:ATTN_IN + CONV_IN], *conv_args)
        four_ctx = fourier_mix(uc[..., ATTN_IN + CONV_IN:], p['w_f'], p['b_f'])
        yc = jnp.concatenate([a_ctx, conv_ctx, four_ctx], axis=-1) @ p['w_out']
        ctx = ctx + mc[5] * yc
        ctx = ffn_sublayer(ctx, mc, 2, p['g_ffn2'], p['w2g'], p['w2u'], p['w2d'])
    x = ffn_sublayer(x, mx, 2, p['g_ffn2'], p['w2g'], p['w2u'], p['w2d'])
    return x, ctx


def setup_inputs(seed: int = 0) -> dict:
    key = jax.random.key(seed)
    ks = jax.random.split(key, 40)
    f32 = jnp.float32

    def nrm(k, shape, scale):
        return jax.random.normal(k, shape, f32) * scale

    def gain(k, shape):
        return 1.0 + 0.02 * jax.random.normal(k, shape, f32)

    L = DEPTH
    return {
        "x": nrm(ks[0], (BATCH, SEQ, D_MODEL), 1.0),
        "c": nrm(ks[1], (BATCH, D_MODEL), 1.0),
        "ctx": nrm(ks[2], (BATCH, CTX_LEN, D_MODEL), 1.0),
        "c_ctx": nrm(ks[3], (D_MODEL,), 1.0),
        "w_ada": nrm(ks[4], (L, D_MODEL, N_MOD * D_MODEL), 0.5 * D_MODEL ** -0.5),
        "b_ada": nrm(ks[5], (L, N_MOD * D_MODEL), 0.01),
        "g_ffn1": gain(ks[6], (L, D_MODEL)),
        "g_mix": gain(ks[7], (L, D_MODEL)),
        "g_ffn2": gain(ks[8], (L, D_MODEL)),
        "w_ffn1_gate": nrm(ks[9], (L, D_MODEL, D_FF), D_MODEL ** -0.5),
        "w_ffn1_up": nrm(ks[10], (L, D_MODEL, D_FF), D_MODEL ** -0.5),
        "w_ffn1_down": nrm(ks[11], (L, D_FF, D_MODEL), D_FF ** -0.5),
        "w_ffn2_gate": nrm(ks[12], (L, D_MODEL, D_FF), D_MODEL ** -0.5),
        "w_ffn2_up": nrm(ks[13], (L, D_MODEL, D_FF), D_MODEL ** -0.5),
        "w_ffn2_down": nrm(ks[14], (L, D_FF, D_MODEL), D_FF ** -0.5),
        "w_in": nrm(ks[15], (L, D_MODEL, IN_WIDTH), D_MODEL ** -0.5),
        "lambda_q1": nrm(ks[16], (L, HEAD_DIM), 0.1),
        "lambda_k1": nrm(ks[17], (L, HEAD_DIM), 0.1),
        "lambda_q2": nrm(ks[18], (L, HEAD_DIM), 0.1),
        "lambda_k2": nrm(ks[19], (L, HEAD_DIM), 0.1),
        "g_subln": gain(ks[20], (L, V_DIM)),
        "w_dw": nrm(ks[21], (L, CONV_K, CONV_WIDTH), CONV_K ** -0.5),
        "b_dw": nrm(ks[22], (L, CONV_WIDTH), 0.01),
        "g_conv_ln": gain(ks[23], (L, CONV_WIDTH)),
        "b_conv_ln": nrm(ks[24], (L, CONV_WIDTH), 0.01),
        "w_pw": nrm(ks[25], (L, CONV_WIDTH, CONV_WIDTH), CONV_WIDTH ** -0.5),
        "b_pw": nrm(ks[26], (L, CONV_WIDTH), 0.01),
        "w_fourier": nrm(ks[27], (L, FOURIER_WIDTH, FOURIER_WIDTH), FOURIER_WIDTH ** -0.5),
        "b_fourier": nrm(ks[28], (L, FOURIER_WIDTH), 0.01),
        "w_out": nrm(ks[29], (L, MIX_WIDTH, D_MODEL), MIX_WIDTH ** -0.5),
        "g_final": gain(ks[30], (D_MODEL,)),
    }


def reference(x, c, ctx, c_ctx, w_ada, b_ada, g_ffn1, g_mix, g_ffn2,
              w_ffn1_gate, w_ffn1_up, w_ffn1_down, w_ffn2_gate, w_ffn2_up, w_ffn2_down,
              w_in, lambda_q1, lambda_k1, lambda_q2, lambda_k2, g_subln,
              w_dw, b_dw, g_conv_ln, b_conv_ln, w_pw, b_pw, w_fourier, b_fourier,
              w_out, g_final):
    n = x.shape[1]
    rope = axial_rope_tables(n)
    for l in range(DEPTH):
        lam_init = 0.8 - 0.6 * math.exp(-0.3 * l)
        mod_x = (jax.nn.silu(c) @ w_ada[l] + b_ada[l])[:, None, :]
        mod_c = (jax.nn.silu(c_ctx) @ w_ada[l] + b_ada[l])[None, None, :]
        p = dict(g_ffn1=g_ffn1[l], g_mix=g_mix[l], g_ffn2=g_ffn2[l],
                 w1g=w_ffn1_gate[l], w1u=w_ffn1_up[l], w1d=w_ffn1_down[l],
                 w2g=w_ffn2_gate[l], w2u=w_ffn2_up[l], w2d=w_ffn2_down[l],
                 w_in=w_in[l], lq1=lambda_q1[l], lk1=lambda_k1[l],
                 lq2=lambda_q2[l], lk2=lambda_k2[l], g_subln=g_subln[l],
                 w_dw=w_dw[l], b_dw=b_dw[l], g_cln=g_conv_ln[l], b_cln=b_conv_ln[l],
                 w_pw=w_pw[l], b_pw=b_pw[l], w_f=w_fourier[l], b_f=b_fourier[l],
                 w_out=w_out[l])
        x, ctx = hybrid_layer(x, ctx, mod_x, mod_c, p, lam_init, rope,
                              need_ctx_out=(l < DEPTH - 1))
    return rmsnorm(x, g_final)
```

```python
import functools
import math

import jax
import jax.numpy as jnp
import numpy as np
from jax import lax
from jax.experimental import pallas as pl
from jax.experimental.pallas import tpu as pltpu

N_MOD = 9
N_HEADS = 4
HEAD_DIM = 64
V_DIM = 2 * HEAD_DIM
QK_WIDTH = N_HEADS * 2 * HEAD_DIM
V_WIDTH = N_HEADS * V_DIM
CONV_WIDTH = 256
FOURIER_WIDTH = 256
FOURIER_GROUPS = 4
FOURIER_GDIM = FOURIER_WIDTH // FOURIER_GROUPS
CONV_K = 31
GRID_W = 64
ROPE_BASE = 10000.0
ROT_AXIS = HEAD_DIM // 2
EPS = 1e-6

V7X_LANES = 128
V7X_VMEM_BYTES = 64 * 1024 * 1024
VMEM_LIMIT_BYTES = V7X_VMEM_BYTES * 7 // 8

CONV_HALO = 16
DFT_Q = 64

F32 = jnp.float32
BF16 = jnp.bfloat16


def _cparams(*sem):
    return pltpu.CompilerParams(dimension_semantics=sem, vmem_limit_bytes=VMEM_LIMIT_BYTES)


def _const_spec(shape):
    zeros = (0,) * len(shape)
    return pl.BlockSpec(shape, lambda *_: zeros)


def _pick_tile(n, target):
    t = min(n, target)
    while n % t:
        t //= 2
    return t


def _silu(x):
    return x * (1.0 / (1.0 + jnp.exp(-x)))


def _sigmoid(x):
    return 1.0 / (1.0 + jnp.exp(-x))


def _norm_modulate(x, g, shift, scale):
    y = x * lax.rsqrt(jnp.mean(x * x, axis=-1, keepdims=True) + EPS)
    return (y * g) * (1.0 + scale) + shift


def _mod_kernel(c_ref, w_ref, b_ref, o_ref):
    a = _silu(c_ref[...]).astype(BF16)
    o_ref[...] = jnp.dot(a, w_ref[...].astype(BF16), preferred_element_type=F32) + b_ref[...]


def _modulation(cc, w_ada_l, b_ada_l):
    rows, d = cc.shape
    width = w_ada_l.shape[1]
    tn = _pick_tile(width, 1024)
    return pl.pallas_call(
        _mod_kernel,
        out_shape=jax.ShapeDtypeStruct((rows, width), F32),
        grid=(width // tn,),
        in_specs=[_const_spec((rows, d)),
                  pl.BlockSpec((d, tn), lambda j: (0, j)),
                  pl.BlockSpec((1, tn), lambda j: (0, j))],
        out_specs=pl.BlockSpec((rows, tn), lambda j: (0, j)),
        compiler_params=_cparams("arbitrary"),
        name="modulation",
    )(cc, w_ada_l, b_ada_l.reshape(1, width))


def _ffn_kernel(x_ref, mod_ref, g_ref, wg_ref, wu_ref, wd_ref, gf_ref, o_ref, acc_ref, *,
                mod_base, n_chunks, final_norm):
    x = x_ref[...]
    mod = mod_ref[0]
    h = _norm_modulate(x, g_ref[...], mod[mod_base:mod_base + 1], mod[mod_base + 1:mod_base + 2])
    h = h.astype(BF16)
    for c in range(n_chunks):
        gate = jnp.dot(h, wg_ref[c], preferred_element_type=F32)
        up = jnp.dot(h, wu_ref[c], preferred_element_type=F32)
        a = (_silu(gate) * up).astype(BF16)
        part = jnp.dot(a, wd_ref[c], preferred_element_type=F32)
        if c == 0:
            acc_ref[...] = part
        else:
            acc_ref[...] += part
    y = x + (0.5 * mod[mod_base + 2:mod_base + 3]) * acc_ref[...]
    if final_norm:
        y = y * lax.rsqrt(jnp.mean(y * y, axis=-1, keepdims=True) + EPS) * gf_ref[...]
    o_ref[...] = y


def _ffn(s, mod, tokens_per_mod, mod_base, g, wg3, wu3, wd3, g_final, final_norm):
    t, d = s.shape
    n_chunks, _, tf = wg3.shape
    tm = _pick_tile(min(t, tokens_per_mod), 512)
    kern = functools.partial(_ffn_kernel, mod_base=mod_base, n_chunks=n_chunks, final_norm=final_norm)
    return pl.pallas_call(
        kern,
        out_shape=jax.ShapeDtypeStruct((t, d), F32),
        grid=(t // tm,),
        in_specs=[pl.BlockSpec((tm, d), lambda i: (i, 0)),
                  pl.BlockSpec((1, N_MOD, d), lambda i: (i * tm // tokens_per_mod, 0, 0)),
                  _const_spec((1, d)),
                  _const_spec((n_chunks, d, tf)),
                  _const_spec((n_chunks, d, tf)),
                  _const_spec((n_chunks, tf, d)),
                  _const_spec((1, d))],
        out_specs=pl.BlockSpec((tm, d), lambda i: (i, 0)),
        scratch_shapes=[pltpu.VMEM((tm, d), F32)],
        compiler_params=_cparams("arbitrary"),
        name="ffn",
    )(s, mod, g.reshape(1, d), wg3, wu3, wd3, g_final.reshape(1, d))


def _rope(u, cos, sin_up, sin_dn):
    return u * cos + pltpu.roll(u, V7X_LANES - ROT_AXIS // 2, 1) * sin_up + pltpu.roll(u, ROT_AXIS // 2, 1) * sin_dn


def _mixin_kernel(x_ref, mod_ref, g_ref, w_ref, cos_ref, sup_ref, sdn_ref, dft_ref, *out_refs,
                  want_q, want_aux):
    out_refs = list(out_refs)
    mod = mod_ref[0]
    h = _norm_modulate(x_ref[...], g_ref[...], mod[3:4], mod[4:5]).astype(BF16)
    cos, sup, sdn = cos_ref[...], sup_ref[...], sdn_ref[...]

    def proj(lo, width):
        return jnp.dot(h, w_ref[:, lo:lo + width], preferred_element_type=F32)

    def roped(lo, scale, ref):
        u = proj(lo, QK_WIDTH)
        for hd in range(N_HEADS):
            blk = slice(hd * V7X_LANES, (hd + 1) * V7X_LANES)
            ref[:, blk] = (_rope(u[:, blk], cos, sup, sdn) * scale).astype(BF16)

    if want_q:
        roped(0, HEAD_DIM ** -0.5, out_refs.pop(0))
    roped(QK_WIDTH, 1.0, out_refs.pop(0))
    out_refs.pop(0)[...] = proj(2 * QK_WIDTH, V_WIDTH).astype(BF16)
    if want_aux:
        lo = 2 * QK_WIDTH + V_WIDTH
        out_refs.pop(0)[...] = proj(lo, CONV_WIDTH) * _sigmoid(proj(lo + CONV_WIDTH, CONV_WIDTH))
        uf = proj(lo + 2 * CONV_WIDTH, FOURIER_WIDTH).astype(BF16)
        f = jnp.dot(uf, dft_ref[...], preferred_element_type=F32)
        out_refs.pop(0)[...] = f[:, :FOURIER_WIDTH].astype(BF16)
        out_refs.pop(0)[...] = f[:, FOURIER_WIDTH:].astype(BF16)


def _mixin(s, mod, tokens_per_mod, g, w_in, rope, seq_len, dft_g, want_q, want_aux):
    t, d = s.shape
    tm = _pick_tile(min(t, tokens_per_mod, seq_len), 512)
    n_pos_tiles = seq_len // tm
    kern = functools.partial(_mixin_kernel, want_q=want_q, want_aux=want_aux)
    row = lambda w: pl.BlockSpec((tm, w), lambda i: (i, 0))
    tab = pl.BlockSpec((tm, V7X_LANES), lambda i: (i % n_pos_tiles, 0))
    out_shape, out_specs = [], []
    if want_q:
        out_shape.append(jax.ShapeDtypeStruct((t, QK_WIDTH), BF16)); out_specs.append(row(QK_WIDTH))
    out_shape += [jax.ShapeDtypeStruct((t, QK_WIDTH), BF16), jax.ShapeDtypeStruct((t, V_WIDTH), BF16)]
    out_specs += [row(QK_WIDTH), row(V_WIDTH)]
    if want_aux:
        out_shape += [jax.ShapeDtypeStruct((t, CONV_WIDTH), F32),
                      jax.ShapeDtypeStruct((t, FOURIER_WIDTH), BF16),
                      jax.ShapeDtypeStruct((t, FOURIER_WIDTH), BF16)]
        out_specs += [row(CONV_WIDTH), row(FOURIER_WIDTH), row(FOURIER_WIDTH)]
    return pl.pallas_call(
        kern,
        out_shape=out_shape,
        grid=(t // tm,),
        in_specs=[row(d),
                  pl.BlockSpec((1, N_MOD, d), lambda i: (i * tm // tokens_per_mod, 0, 0)),
                  _const_spec((1, d)),
                  _const_spec(w_in.shape),
                  tab, tab, tab,
                  _const_spec(dft_g.shape)],
        out_specs=out_specs,
        compiler_params=_cparams("arbitrary"),
        name="mixin",
    )(s, mod, g.reshape(1, d), w_in, *rope, dft_g)


def _attn_kernel(*refs, tq, tk, n_lat_chunks, lam_init):
    refs = list(refs)
    q_ref = refs.pop(0)
    kl_ref = vl_ref = None
    if n_lat_chunks:
        kl_ref, vl_ref = refs.pop(0), refs.pop(0)
    kc_ref, vc_ref, lq1, lk1, lq2, lk2, g_ref, o_ref, qs_ref, m_ref, l_ref, acc_ref = refs

    q = q_ref[0]
    lane = lax.broadcasted_iota(jnp.int32, q.shape, 1)
    zero = jnp.zeros_like(q)
    qs_ref[0:tq] = jnp.where(lane < HEAD_DIM, q, zero)
    qs_ref[tq:2 * tq] = jnp.where(lane >= HEAD_DIM, q, zero)
    m_ref[...] = jnp.full(m_ref.shape, -jnp.inf, F32)
    l_ref[...] = jnp.zeros(l_ref.shape, F32)
    acc_ref[...] = jnp.zeros(acc_ref.shape, F32)

    def step(k, v):
        s = lax.dot_general(qs_ref[...], k, (((1,), (1,)), ((), ())), preferred_element_type=F32)
        m_prev = m_ref[...]
        m_new = jnp.maximum(m_prev, jnp.max(s, axis=-1, keepdims=True))
        alpha = jnp.exp(m_prev - m_new)
        p = jnp.exp(s - m_new)
        l_ref[...] = alpha * l_ref[...] + jnp.sum(p, axis=-1, keepdims=True)
        acc_ref[...] = alpha * acc_ref[...] + jnp.dot(p.astype(BF16), v, preferred_element_type=F32)
        m_ref[...] = m_new

    if n_lat_chunks:
        def body(j, carry):
            start = pl.multiple_of(j * tk, tk)
            step(kl_ref[0, pl.ds(start, tk), :], vl_ref[0, pl.ds(start, tk), :])
            return carry
        lax.fori_loop(0, n_lat_chunks, body, 0)
    step(kc_ref[0], vc_ref[0])

    lam = (jnp.exp(jnp.sum(lq1[...] * lk1[...], axis=-1, keepdims=True))
           - jnp.exp(jnp.sum(lq2[...] * lk2[...], axis=-1, keepdims=True)) + lam_init)
    o = acc_ref[...] * (1.0 / l_ref[...])
    o = o[0:tq] - lam * o[tq:2 * tq]
    o = o * lax.rsqrt(jnp.mean(o * o, axis=-1, keepdims=True) + EPS) * g_ref[...] * (1.0 - lam_init)
    o_ref[0] = o.astype(BF16)


def _attention(q, k_lat, v_lat, k_ctx, v_ctx, lam_params, g_subln, lam_init):
    b, nq, _ = q.shape
    lc = k_ctx.shape[1]
    tq = _pick_tile(nq, 256)
    head = lambda n: pl.BlockSpec((1, n, V7X_LANES), lambda bi, hi, i: (bi, 0, hi))
    qspec = pl.BlockSpec((1, tq, V7X_LANES), lambda bi, hi, i: (bi, i, hi))
    in_specs, args = [qspec], [q]
    tk, n_lat_chunks = 0, 0
    if k_lat is not None:
        nk = k_lat.shape[1]
        tk = _pick_tile(nk, 512)
        n_lat_chunks = nk // tk
        in_specs += [head(nk), head(nk)]
        args += [k_lat, v_lat]
    in_specs += [head(lc), head(lc)] + [_const_spec((1, HEAD_DIM))] * 4 + [_const_spec((1, V_DIM))]
    args += [k_ctx, v_ctx] + [p.reshape(1, HEAD_DIM) for p in lam_params] + [g_subln.reshape(1, V_DIM)]
    kern = functools.partial(_attn_kernel, tq=tq, tk=tk, n_lat_chunks=n_lat_chunks, lam_init=lam_init)
    return pl.pallas_call(
        kern,
        out_shape=jax.ShapeDtypeStruct((b, nq, V_WIDTH), BF16),
        grid=(b, N_HEADS, nq // tq),
        in_specs=in_specs,
        out_specs=qspec,
        scratch_shapes=[pltpu.VMEM((2 * tq, V7X_LANES), BF16),
                        pltpu.VMEM((2 * tq, 1), F32),
                        pltpu.VMEM((2 * tq, 1), F32),
                        pltpu.VMEM((2 * tq, V_DIM), F32)],
        compiler_params=_cparams("arbitrary", "arbitrary", "arbitrary"),
        name="diff_attention",
    )(*args)


def _conv_kernel(prev_ref, cur_ref, next_ref, wdw_ref, bdw_ref, g_ref, b_ref, wpw_ref, bpw_ref, o_ref, ext_ref, *,
                 tn, n_tiles):
    i = pl.program_id(1)
    halo = jnp.zeros((CONV_HALO, CONV_WIDTH), F32)
    ext_ref[0:CONV_HALO] = jnp.where(i > 0, prev_ref[0], halo)
    ext_ref[CONV_HALO:CONV_HALO + tn] = cur_ref[0]
    ext_ref[CONV_HALO + tn:2 * CONV_HALO + tn] = jnp.where(i < n_tiles - 1, next_ref[0], halo)
    base = CONV_HALO - CONV_K // 2
    y = jnp.zeros((tn, CONV_WIDTH), F32)
    for j in range(CONV_K):
        y = y + wdw_ref[j:j + 1, :] * ext_ref[base + j:base + j + tn, :]
    y = y + bdw_ref[...]
    mu = jnp.mean(y, axis=-1, keepdims=True)
    yc = y - mu
    var = jnp.mean(yc * yc, axis=-1, keepdims=True)
    y = _silu(yc * lax.rsqrt(var + EPS) * g_ref[...] + b_ref[...]).astype(BF16)
    o_ref[0] = (jnp.dot(y, wpw_ref[...], preferred_element_type=F32) + bpw_ref[...]).astype(BF16)


def _conv_module(glu, w_dw, b_dw, g_ln, b_ln, w_pw, b_pw):
    b, n, w = glu.shape
    tn = _pick_tile(n, 512)
    n_tiles = n // tn
    per = tn // CONV_HALO
    last = n // CONV_HALO - 1
    vec = lambda a: a.reshape(1, w)
    kern = functools.partial(_conv_kernel, tn=tn, n_tiles=n_tiles)
    return pl.pallas_call(
        kern,
        out_shape=jax.ShapeDtypeStruct((b, n, w), BF16),
        grid=(b, n_tiles),
        in_specs=[pl.BlockSpec((1, CONV_HALO, w), lambda bi, i: (bi, jnp.maximum(i * per - 1, 0), 0)),
                  pl.BlockSpec((1, tn, w), lambda bi, i: (bi, i, 0)),
                  pl.BlockSpec((1, CONV_HALO, w), lambda bi, i: (bi, jnp.minimum((i + 1) * per, last), 0)),
                  _const_spec((CONV_K + 1, w)),
                  _const_spec((1, w)), _const_spec((1, w)), _const_spec((1, w)),
                  _const_spec((w, w)), _const_spec((1, w))],
        out_specs=pl.BlockSpec((1, tn, w), lambda bi, i: (bi, i, 0)),
        scratch_shapes=[pltpu.VMEM((tn + 2 * CONV_HALO, w), F32)],
        compiler_params=_cparams("arbitrary", "arbitrary"),
        name="conv_module",
    )(glu, glu, glu, jnp.pad(w_dw, ((0, 1), (0, 0))), vec(b_dw), vec(g_ln), vec(b_ln), w_pw, vec(b_pw))


def _cmm_kernel(*refs, want_imag, twiddle):
    refs = list(refs)
    mc_ref, ms_ref, xr_ref, xi_ref = refs[:4]
    refs = refs[4:]
    mc, ms, xr, xi = mc_ref[...], ms_ref[...], xr_ref[0], xi_ref[0]
    dot = lambda a, b: jnp.dot(a, b, preferred_element_type=F32)
    yr = dot(mc, xr) + dot(ms, xi)
    yi = dot(mc, xi) - dot(ms, xr) if want_imag else None
    if twiddle:
        tc, ts = refs.pop(0)[...], refs.pop(0)[...]
        yr, yi = yr * tc + yi * ts, yi * tc - yr * ts
    yr_ref = refs.pop(0)
    yr_ref[0] = yr.astype(yr_ref.dtype)
    if want_imag:
        yi_ref = refs.pop(0)
        yi_ref[0] = yi.astype(yi_ref.dtype)


def _cmm(mc, ms, xr, xi, tw, want_imag):
    m, k = mc.shape
    b, _, cols = xr.shape
    tc = _pick_tile(cols, 2048)
    xspec = pl.BlockSpec((1, k, tc), lambda bi, j: (bi, 0, j))
    ospec = pl.BlockSpec((1, m, tc), lambda bi, j: (bi, 0, j))
    in_specs = [_const_spec((m, k)), _const_spec((m, k)), xspec, xspec]
    args = [mc, ms, xr, xi]
    if tw is not None:
        tspec = pl.BlockSpec((m, tc), lambda bi, j: (0, j))
        in_specs += [tspec, tspec]
        args += list(tw)
    if want_imag:
        out_shape = [jax.ShapeDtypeStruct((b, m, cols), BF16)] * 2
        out_specs = [ospec, ospec]
    else:
        out_shape = jax.ShapeDtypeStruct((b, m, cols), F32)
        out_specs = ospec
    kern = functools.partial(_cmm_kernel, want_imag=want_imag, twiddle=tw is not None)
    return pl.pallas_call(
        kern, out_shape=out_shape, grid=(b, cols // tc), in_specs=in_specs, out_specs=out_specs,
        compiler_params=_cparams("arbitrary", "arbitrary"),
        name="dft_stage",
    )(*args)


def _dft_tables(n, scale):
    idx = (jnp.arange(n, dtype=jnp.int32)[:, None] * jnp.arange(n, dtype=jnp.int32)[None, :]) % n
    ang = idx.astype(F32) * (2.0 * math.pi / n)
    return (jnp.cos(ang) * scale).astype(BF16), (jnp.sin(ang) * scale).astype(BF16)


def _position_dft_real(fr, fi):
    b, n, c = fr.shape
    scale = (n * FOURIER_GDIM) ** -0.5
    if n % DFT_Q or n // DFT_Q < 16:
        mc, ms = _dft_tables(n, scale)
        return _cmm(mc, ms, fr, fi, None, False)
    p, q = n // DFT_Q, DFT_Q
    c1, s1 = _dft_tables(p, 1.0)
    c2, s2 = _dft_tables(q, scale)
    tw_idx = jnp.arange(p, dtype=jnp.int32)[:, None] * jnp.arange(q, dtype=jnp.int32)[None, :]
    tw_ang = tw_idx.astype(F32) * (2.0 * math.pi / n)
    tw = tuple(jnp.repeat(t, c, axis=1) for t in (jnp.cos(tw_ang), jnp.sin(tw_ang)))
    zr, zi = _cmm(c1, s1, fr.reshape(b, p, q * c), fi.reshape(b, p, q * c), tw, True)
    swap = lambda z: z.reshape(b, p, q, c).transpose(0, 2, 1, 3).reshape(b, q, p * c)
    y = _cmm(c2, s2, swap(zr), swap(zi), None, False)
    return y.reshape(b, n, c)


def _mixout_kernel(x_ref, mod_ref, a_ref, cv_ref, f_ref, wf_ref, bf_ref, wo_ref, o_ref):
    dot = lambda a, b: jnp.dot(a, b, preferred_element_type=F32)
    fo = (dot(f_ref[...].astype(BF16), wf_ref[...]) + bf_ref[...]).astype(BF16)
    y = dot(a_ref[...], wo_ref[0:V_WIDTH, :])
    y = y + dot(cv_ref[...], wo_ref[V_WIDTH:V_WIDTH + CONV_WIDTH, :])
    y = y + dot(fo, wo_ref[V_WIDTH + CONV_WIDTH:, :])
    o_ref[...] = x_ref[...] + mod_ref[0][5:6] * y


def _mixout(s, mod, tokens_per_mod, a, cv, f, w_f, b_f, w_out):
    t, d = s.shape
    tm = _pick_tile(min(t, tokens_per_mod), 512)
    row = lambda w: pl.BlockSpec((tm, w), lambda i: (i, 0))
    return pl.pallas_call(
        _mixout_kernel,
        out_shape=jax.ShapeDtypeStruct((t, d), F32),
        grid=(t // tm,),
        in_specs=[row(d),
                  pl.BlockSpec((1, N_MOD, d), lambda i: (i * tm // tokens_per_mod, 0, 0)),
                  row(V_WIDTH), row(CONV_WIDTH), row(FOURIER_WIDTH),
                  _const_spec(w_f.shape), _const_spec((1, FOURIER_WIDTH)), _const_spec(w_out.shape)],
        out_specs=row(d),
        compiler_params=_cparams("arbitrary"),
        name="mixout",
    )(s, mod, a, cv, f, w_f, b_f.reshape(1, FOURIER_WIDTH), w_out)


def _rope_tables(n):
    rows = n // GRID_W
    row = jnp.broadcast_to(jnp.arange(rows)[:, None], (rows, GRID_W)).reshape(-1).astype(F32)
    col = jnp.broadcast_to(jnp.arange(GRID_W)[None, :], (rows, GRID_W)).reshape(-1).astype(F32)
    half = ROT_AXIS // 2
    inv = ROPE_BASE ** (-jnp.arange(0, ROT_AXIS, 2, dtype=F32) / ROT_AXIS)
    ang_r, ang_c = row[:, None] * inv, col[:, None] * inv
    zeros = jnp.zeros((n, half), F32)
    cos = jnp.concatenate([jnp.cos(ang_r)] * 2 + [jnp.cos(ang_c)] * 2, axis=1)
    sup = jnp.concatenate([-jnp.sin(ang_r), zeros, -jnp.sin(ang_c), zeros], axis=1)
    sdn = jnp.concatenate([zeros, jnp.sin(ang_r), zeros, jnp.sin(ang_c)], axis=1)
    reps = V7X_LANES // HEAD_DIM
    return tuple(jnp.tile(t, (1, reps)) for t in (cos, sup, sdn))


def _identity_rope_tables(n):
    return (jnp.ones((n, V7X_LANES), F32), jnp.zeros((n, V7X_LANES), F32), jnp.zeros((n, V7X_LANES), F32))


def _group_dft_matrix():
    idx = np.outer(np.arange(FOURIER_GDIM), np.arange(FOURIER_GDIM)) % FOURIER_GDIM
    ang = 2.0 * np.pi * idx / FOURIER_GDIM
    eye = np.eye(FOURIER_GROUPS)
    m = np.concatenate([np.kron(eye, np.cos(ang)), np.kron(eye, -np.sin(ang))], axis=1)
    return jnp.asarray(m, dtype=BF16)


def _chunk_cols(w, tf):
    d, f = w.shape
    return w.reshape(d, f // tf, tf).transpose(1, 0, 2).astype(BF16)


def kernel(x, c, ctx, c_ctx, w_ada, b_ada, g_ffn1, g_mix, g_ffn2, w_ffn1_gate, w_ffn1_up, w_ffn1_down,
           w_ffn2_gate, w_ffn2_up, w_ffn2_down, w_in, lambda_q1, lambda_k1, lambda_q2, lambda_k2, g_subln,
           w_dw, b_dw, g_conv_ln, b_conv_ln, w_pw, b_pw, w_fourier, b_fourier, w_out, g_final):
    b, n, d = x.shape
    lc = ctx.shape[1]
    depth = w_ada.shape[0]
    d_ff = w_ffn1_gate.shape[2]
    tf = _pick_tile(d_ff, 256)

    rope_lat = _rope_tables(n)
    rope_ctx = _identity_rope_tables(lc)
    dft_g = _group_dft_matrix()
    cc = jnp.zeros((8, d), F32).at[:b].set(c).at[b].set(c_ctx)

    xs = x.reshape(b * n, d)
    cs = ctx.reshape(b * lc, d)
    for l in range(depth):
        last = l == depth - 1
        lam_init = 0.8 - 0.6 * math.exp(-0.3 * l)
        mod = _modulation(cc, w_ada[l], b_ada[l]).reshape(8, N_MOD, d)
        mod_x, mod_c = mod[:b], mod[b:b + 1]
        ffn1 = (g_ffn1[l], _chunk_cols(w_ffn1_gate[l], tf), _chunk_cols(w_ffn1_up[l], tf),
                w_ffn1_down[l].reshape(d_ff // tf, tf, d).astype(BF16))
        ffn2 = (g_ffn2[l], _chunk_cols(w_ffn2_gate[l], tf), _chunk_cols(w_ffn2_up[l], tf),
                w_ffn2_down[l].reshape(d_ff // tf, tf, d).astype(BF16))
        w_in_l, w_out_l = w_in[l].astype(BF16), w_out[l].astype(BF16)
        w_pw_l, w_f_l = w_pw[l].astype(BF16), w_fourier[l].astype(BF16)
        lam_params = (lambda_q1[l], lambda_k1[l], lambda_q2[l], lambda_k2[l])
        conv_args = (w_dw[l], b_dw[l], g_conv_ln[l], b_conv_ln[l], w_pw_l, b_pw[l])

        xs = _ffn(xs, mod_x, n, 0, *ffn1, g_final, False)
        cs = _ffn(cs, mod_c, b * lc, 0, *ffn1, g_final, False)

        q, k, v, glu, fr, fi = _mixin(xs, mod_x, n, g_mix[l], w_in_l, rope_lat, n, dft_g, True, True)
        if last:
            kc, vc = _mixin(cs, mod_c, b * lc, g_mix[l], w_in_l, rope_ctx, lc, dft_g, False, False)
        else:
            qc, kc, vc, gluc, frc, fic = _mixin(cs, mod_c, b * lc, g_mix[l], w_in_l, rope_ctx, lc, dft_g,
                                                True, True)
        r3 = lambda a, rows: a.reshape(b, rows, a.shape[-1])
        kc3, vc3 = r3(kc, lc), r3(vc, lc)
        a_lat = _attention(r3(q, n), r3(k, n), r3(v, n), kc3, vc3, lam_params, g_subln[l], lam_init)
        cv_lat = _conv_module(r3(glu, n), *conv_args)
        f_lat = _position_dft_real(r3(fr, n), r3(fi, n))
        xs = _mixout(xs, mod_x, n, a_lat.reshape(b * n, -1), cv_lat.reshape(b * n, -1),
                     f_lat.reshape(b * n, -1), w_f_l, b_fourier[l], w_out_l)
        if not last:
            a_ctx = _attention(r3(qc, lc), None, None, kc3, vc3, lam_params, g_subln[l], lam_init)
            cv_ctx = _conv_module(r3(gluc, lc), *conv_args)
            f_ctx = _position_dft_real(r3(frc, lc), r3(fic, lc))
            cs = _mixout(cs, mod_c, b * lc, a_ctx.reshape(b * lc, -1), cv_ctx.reshape(b * lc, -1),
                         f_ctx.reshape(b * lc, -1), w_f_l, b_fourier[l], w_out_l)
            cs = _ffn(cs, mod_c, b * lc, 6, *ffn2, g_final, False)
        xs = _ffn(xs, mod_x, n, 6, *ffn2, g_final, last)
    return xs.reshape(b, n, d)
```
